```python
import jax, jax.numpy as jnp
from jax import lax
import numpy as np

D_MODEL = 2048
BATCH = 2
SEQ = 16384
DEPTH = 1

ROPE_THETA = 500000.0
NORM_EPS = 1e-6
BLOCK = 128
MAX_POS_OFFSET = 1024

MLA_HEADS = D_MODEL // 128
MLA_NOPE = 128
MLA_ROPE = 64
MLA_V = 128
MLA_QK = MLA_NOPE + MLA_ROPE
MLA_KV_RANK = D_MODEL // 4
MLA_WIDTH = MLA_HEADS * MLA_V

DIL_GROUPS = ((128, 1), (512, 4), (2048, 16))
DIL_N_GROUPS = 3
DIL_HEADS = D_MODEL // 256
DIL_HEAD_DIM = 128
DIL_ROT = DIL_HEAD_DIM // 4
DIL_WIDTH = DIL_HEADS * DIL_HEAD_DIM
DIL_QKV = 3 * DIL_N_GROUPS * DIL_WIDTH

IN_SIZES = (MLA_HEADS * MLA_QK, MLA_KV_RANK + MLA_ROPE, MLA_WIDTH, DIL_QKV, DIL_WIDTH, D_MODEL, D_MODEL)
IN_SPLITS = (3072, 3648, 5696, 14912, 15936, 17984)
IN_TOTAL = 20032

kernel_name = 'hybrid_mla_dilated_gated_block'


def rms_norm(x, g):
    xf = x.astype(jnp.float32)
    y = xf * lax.rsqrt(jnp.mean(jnp.square(xf), axis=-1, keepdims=True) + NORM_EPS)
    return (y * g.astype(jnp.float32)).astype(x.dtype)


def rotary(x, positions):
    r = x.shape[-1]
    inv_freq = 1.0 / (ROPE_THETA ** (jnp.arange(0, r, 2, dtype=jnp.float32) / r))
    ang = positions.astype(jnp.float32)[:, :, None, None] * inv_freq
    cos, sin = jnp.cos(ang), jnp.sin(ang)
    xf = x.astype(jnp.float32)
    x1, x2 = xf[..., : r // 2], xf[..., r // 2:]
    return jnp.concatenate([x1 * cos - x2 * sin, x2 * cos + x1 * sin], axis=-1).astype(x.dtype)


def partial_rotary(x, positions):
    return jnp.concatenate([rotary(x[..., :DIL_ROT], positions), x[..., DIL_ROT:]], axis=-1)


def causal_dense_attention(q, k, v):
    B, S, H, Dk = q.shape
    n_blk = S // BLOCK
    scale = Dk ** -0.5
    qb = jnp.moveaxis(q.reshape(B, n_blk, BLOCK, H, Dk), 1, 0)
    k_pos = jnp.arange(S)

    def one_block(args):
        q_blk, blk = args
        s = jnp.einsum('bqhd,bkhd->bhqk', q_blk, k, preferred_element_type=jnp.float32) * scale
        q_pos = blk * BLOCK + jnp.arange(BLOCK)
        s = jnp.where(k_pos[None, :] <= q_pos[:, None], s, -jnp.inf)
        p = jax.nn.softmax(s, axis=-1)
        return jnp.einsum('bhqk,bkhd->bqhd', p.astype(v.dtype), v)

    out = lax.map(one_block, (qb, jnp.arange(n_blk)))
    return jnp.moveaxis(out, 0, 1).reshape(B, S, H, v.shape[-1])


def dilated_group_attention(q, k, v, window, dilation):
    B, S, H, D = q.shape
    n_steps = window // dilation
    span = dilation * BLOCK
    s_pad = -(-S // span) * span
    n_blk = s_pad // span

    def to_blocks(t):
        t = jnp.pad(t, ((0, 0), (0, s_pad - S), (0, 0), (0, 0)))
        t = jnp.swapaxes(t.reshape(B, s_pad // dilation, dilation, H, D), 1, 2)
        return t.reshape(B, dilation, n_blk, BLOCK, H, D)

    def with_prev(t):
        prev = jnp.pad(t[:, :, :-1], ((0, 0), (0, 0), (1, 0), (0, 0), (0, 0), (0, 0)))
        return jnp.concatenate([prev, t], axis=3)

    qb = to_blocks(q)
    kb = with_prev(to_blocks(k))
    vb = with_prev(to_blocks(v))
    s = jnp.einsum('brnqhd,brnkhd->brnhqk', qb, kb, preferred_element_type=jnp.float32) * (D ** -0.5)
    q_idx = BLOCK + jnp.arange(BLOCK)[:, None]
    k_idx = jnp.arange(2 * BLOCK)[None, :]
    rel = q_idx - k_idx
    band = (rel >= 0) & (rel <= n_steps)
    has_prev = (jnp.arange(n_blk) > 0)[:, None, None] | (k_idx >= BLOCK)[None]
    valid = band[None] & has_prev
    s = jnp.where(valid[None, None, :, None], s, -jnp.inf)
    m = jnp.max(s, axis=-1, keepdims=True)
    e = jnp.exp(s - m)
    l = jnp.sum(e, axis=-1, keepdims=True)
    o = jnp.einsum('brnhqk,brnkhd->brnqhd', (e / l).astype(v.dtype), vb)
    lse = jnp.swapaxes((m + jnp.log(l))[..., 0], 3, 4)

    def from_blocks(t):
        t = t.reshape((B, dilation, s_pad // dilation) + t.shape[4:])
        t = jnp.swapaxes(t, 1, 2)
        return t.reshape((B, s_pad) + t.shape[3:])[:, :S]

    return from_blocks(o), from_blocks(lse)


def mla_branch(q_all, kv_a, positions, g_kv, w_kv_b):
    B, S, _ = q_all.shape
    q = q_all.reshape(B, S, MLA_HEADS, MLA_QK)
    q = jnp.concatenate([q[..., :MLA_NOPE], rotary(q[..., MLA_NOPE:], positions)], axis=-1)
    c_kv = rms_norm(kv_a[..., :MLA_KV_RANK], g_kv)
    k_pe = rotary(kv_a[..., MLA_KV_RANK:][:, :, None, :], positions)
    kv = jnp.einsum('bsr,rhe->bshe', c_kv, w_kv_b.reshape(MLA_KV_RANK, MLA_HEADS, MLA_NOPE + MLA_V))
    k = jnp.concatenate([kv[..., :MLA_NOPE], jnp.broadcast_to(k_pe, (B, S, MLA_HEADS, MLA_ROPE))], axis=-1)
    v = kv[..., MLA_NOPE:]
    return causal_dense_attention(q, k, v).reshape(B, S, MLA_WIDTH)


def dilated_branch(qkv, positions):
    B, S, _ = qkv.shape
    qkv = qkv.reshape(B, S, 3, DIL_N_GROUPS * DIL_HEADS, DIL_HEAD_DIM)
    q = partial_rotary(qkv[:, :, 0], positions).reshape(B, S, DIL_N_GROUPS, DIL_HEADS, DIL_HEAD_DIM)
    k = partial_rotary(qkv[:, :, 1], positions).reshape(B, S, DIL_N_GROUPS, DIL_HEADS, DIL_HEAD_DIM)
    v = qkv[:, :, 2].reshape(B, S, DIL_N_GROUPS, DIL_HEADS, DIL_HEAD_DIM)
    outs, lses = [], []
    for g, (window, dilation) in enumerate(DIL_GROUPS):
        o, lse = dilated_group_attention(q[:, :, g], k[:, :, g], v[:, :, g], window, dilation)
        outs.append(o)
        lses.append(lse)
    wts = jax.nn.softmax(jnp.stack(lses, axis=0), axis=0)
    o = jnp.sum(wts[..., None] * jnp.stack(outs, axis=0).astype(jnp.float32), axis=0)
    return o.astype(qkv.dtype).reshape(B, S, DIL_WIDTH)


def setup_inputs(seed: int = 0) -> dict:
    key = jax.random.key(seed)
    ks = jax.random.split(key, 16)

    def nrm(k, shape, fan_in):
        return jax.random.normal(k, shape, jnp.float32) * (fan_in ** -0.5)

    x = jax.random.normal(ks[0], (BATCH, SEQ, D_MODEL), jnp.float32)
    c = jax.random.normal(ks[1], (BATCH, D_MODEL), jnp.float32)
    offset = jax.random.randint(ks[2], (BATCH, 1), 0, MAX_POS_OFFSET, dtype=jnp.int32)
    positions = jnp.arange(SEQ, dtype=jnp.int32)[None, :] + offset
    w_ada = nrm(ks[3], (DEPTH, D_MODEL, 3 * D_MODEL), D_MODEL)
    b_ada = 0.02 * jax.random.normal(ks[4], (DEPTH, 3 * D_MODEL), jnp.float32)
    g_pre = 1.0 + 0.02 * jax.random.normal(ks[5], (DEPTH, D_MODEL), jnp.float32)
    g_post = 1.0 + 0.02 * jax.random.normal(ks[6], (DEPTH, D_MODEL), jnp.float32)
    w_in = nrm(ks[7], (DEPTH, D_MODEL, IN_TOTAL), D_MODEL)
    g_kv = 1.0 + 0.02 * jax.random.normal(ks[8], (DEPTH, MLA_KV_RANK), jnp.float32)
    w_kv_b = nrm(ks[9], (DEPTH, MLA_KV_RANK, MLA_HEADS * (MLA_NOPE + MLA_V)), MLA_KV_RANK)
    w_mla_proj = nrm(ks[10], (DEPTH, MLA_WIDTH, D_MODEL), MLA_WIDTH)
    w_dil_proj = nrm(ks[11], (DEPTH, DIL_WIDTH, D_MODEL), DIL_WIDTH)
    w_o = nrm(ks[12], (DEPTH, D_MODEL, D_MODEL), D_MODEL)
    return {'x': x, 'c': c, 'positions': positions, 'w_ada': w_ada, 'b_ada': b_ada,
            'g_pre': g_pre, 'g_post': g_post, 'w_in': w_in, 'g_kv': g_kv, 'w_kv_b': w_kv_b,
            'w_mla_proj': w_mla_proj, 'w_dil_proj': w_dil_proj, 'w_o': w_o}


def reference(x, c, positions, w_ada, b_ada, g_pre, g_post, w_in, g_kv, w_kv_b,
              w_mla_proj, w_dil_proj, w_o):
    for layer in range(DEPTH):
        mod = jnp.einsum('bd,de->be', jax.nn.silu(c), w_ada[layer]) + b_ada[layer]
        shift, scale, gate = jnp.split(mod, 3, axis=-1)
        h = rms_norm(x, g_pre[layer]) * (1.0 + scale[:, None, :]) + shift[:, None, :]
        proj = jnp.einsum('bsd,de->bse', h, w_in[layer])
        q_mla, kv_a, z_mla, qkv_dil, z_dil, gl_mla, gl_dil = jnp.split(proj, IN_SPLITS, axis=-1)
        a = mla_branch(q_mla, kv_a, positions, g_kv[layer], w_kv_b[layer]) * jax.nn.silu(z_mla)
        y_mla = jnp.einsum('bse,ed->bsd', a, w_mla_proj[layer])
        bb = dilated_branch(qkv_dil, positions) * jax.nn.silu(z_dil)
        y_dil = jnp.einsum('bse,ed->bsd', bb, w_dil_proj[layer])
        merged = jax.nn.sigmoid(gl_mla) * y_mla + jax.nn.sigmoid(gl_dil) * y_dil
        out = jnp.einsum('bsd,de->bse', merged, w_o[layer])
        x = x + gate[:, None, :] * rms_norm(out, g_post[layer])
    return x
```

```python
import functools
import math

import numpy as np
import jax
import jax.numpy as jnp
from jax import lax
from jax.experimental import pallas as pl
from jax.experimental.pallas import tpu as pltpu

D_MODEL = 2048
ROPE_THETA = 500000.0
NORM_EPS = 1e-6

MLA_HEADS = 16
MLA_NOPE = 128
MLA_ROPE = 64
MLA_V = 128
MLA_QK = MLA_NOPE + MLA_ROPE
MLA_KV_RANK = 512
MLA_KPAD = 256

DIL_GROUPS = ((128, 1), (512, 4), (2048, 16))
DIL_N_GROUPS = 3
DIL_HEADS = 8
DIL_HEAD_DIM = 128
DIL_ROT = 32
DIL_WIDTH = DIL_HEADS * DIL_HEAD_DIM
DIL_BLOCK = 128

LANES = 128
HALF = LANES // 2
NEG = -1e30
LOG2E = math.log2(math.e)
VMEM_LIMIT = 56 * 1024 * 1024

F32 = jnp.float32
BF16 = jnp.bfloat16


def _cparams(sem):
    return pltpu.CompilerParams(dimension_semantics=sem, vmem_limit_bytes=VMEM_LIMIT)


def _in_proj_columns():
    q0 = 0
    kv0 = MLA_HEADS * MLA_QK
    z_mla0 = kv0 + MLA_KV_RANK + MLA_ROPE
    dil0 = z_mla0 + MLA_HEADS * MLA_V
    z_dil0 = dil0 + 3 * DIL_N_GROUPS * DIL_WIDTH
    gl_mla0 = z_dil0 + DIL_WIDTH
    gl_dil0 = gl_mla0 + D_MODEL
    half = MLA_ROPE // 2

    q_cols = []
    for h in range(MLA_HEADS):
        q_cols.append(q0 + h * MLA_QK + np.arange(MLA_NOPE))
    for j in range(MLA_HEADS // 2):
        a, b = 2 * j, 2 * j + 1
        for part in (0, 1):
            for hh in (a, b):
                q_cols.append(q0 + hh * MLA_QK + MLA_NOPE + part * half + np.arange(half))
    q_cols = np.concatenate(q_cols)

    kx1 = kv0 + MLA_KV_RANK + np.arange(half)
    kx2 = kx1 + half
    kv_cols = np.concatenate([kv0 + np.arange(MLA_KV_RANK), kx1, kx1, kx2, kx2])

    g_cols = np.concatenate([z_mla0 + np.arange(MLA_HEADS * MLA_V), gl_mla0 + np.arange(D_MODEL),
                             gl_dil0 + np.arange(D_MODEL), z_dil0 + np.arange(DIL_WIDTH)])

    rh = DIL_ROT // 2
    perm = np.concatenate([np.arange(0, rh), np.arange(DIL_ROT, DIL_ROT + HALF - rh),
                           np.arange(rh, DIL_ROT), np.arange(DIL_ROT + HALF - rh, DIL_HEAD_DIM)])
    n_heads = DIL_N_GROUPS * DIL_HEADS
    d_cols = []
    for part in range(3):
        for hh in range(n_heads):
            base = dil0 + (part * n_heads + hh) * DIL_HEAD_DIM
            d_cols.append(base + (perm if part < 2 else np.arange(DIL_HEAD_DIM)))
    d_cols = np.concatenate(d_cols)
    return q_cols, kv_cols, g_cols, d_cols


def _rope_lane_rows():
    inv64 = 1.0 / (ROPE_THETA ** (jnp.arange(0, MLA_ROPE, 2, dtype=F32) / MLA_ROPE))
    inv32 = 1.0 / (ROPE_THETA ** (jnp.arange(0, DIL_ROT, 2, dtype=F32) / DIL_ROT))
    mla_f = jnp.tile(inv64, 4)[None, :]
    z = jnp.zeros((HALF - DIL_ROT // 2,), F32)
    dil_f = jnp.concatenate([inv32, z, inv32, z])[None, :]
    sign = jnp.concatenate([-jnp.ones((HALF,), F32), jnp.ones((HALF,), F32)])[None, :]
    return mla_f, dil_f, sign


def _ada_kernel(c_ref, w_ref, b_ref, o_ref):
    c = c_ref[...]
    sc = c * (1.0 / (1.0 + jnp.exp(-c)))
    o_ref[...] = jnp.dot(sc, w_ref[...], preferred_element_type=F32) + b_ref[...]


def _ada_call(c_pad, w_ada, b_ada, tn=512):
    rows, d = c_pad.shape
    n = w_ada.shape[1]
    return pl.pallas_call(
        _ada_kernel,
        grid=(n // tn,),
        in_specs=[pl.BlockSpec((rows, d), lambda j: (0, 0)),
                  pl.BlockSpec((d, tn), lambda j: (0, j)),
                  pl.BlockSpec((1, tn), lambda j: (0, j))],
        out_specs=pl.BlockSpec((rows, tn), lambda j: (0, j)),
        out_shape=jax.ShapeDtypeStruct((rows, n), F32),
        compiler_params=_cparams(("arbitrary",)),
        name="ada",
    )(c_pad, w_ada, b_ada)


def _tables_kernel(pos_ref, mf_ref, df_ref, sg_ref, mc_ref, ms_ref, dc_ref, ds_ref):
    pos = pos_ref[...].astype(F32)
    sg = sg_ref[...]
    ang = pos * mf_ref[...]
    mc_ref[...] = jnp.cos(ang)
    ms_ref[...] = jnp.sin(ang) * sg
    ang = pos * df_ref[...]
    dc_ref[...] = jnp.cos(ang)
    ds_ref[...] = jnp.sin(ang) * sg


def _tables_call(pos_col, tm=1024):
    m = pos_col.shape[0]
    mla_f, dil_f, sign = _rope_lane_rows()
    row = pl.BlockSpec((1, LANES), lambda i: (0, 0))
    tab = pl.BlockSpec((tm, LANES), lambda i: (i, 0))
    shp = jax.ShapeDtypeStruct((m, LANES), F32)
    return pl.pallas_call(
        _tables_kernel,
        grid=(m // tm,),
        in_specs=[pl.BlockSpec((tm, 1), lambda i: (i, 0)), row, row, row],
        out_specs=[tab, tab, tab, tab],
        out_shape=[shp, shp, shp, shp],
        compiler_params=_cparams(("arbitrary",)),
        name="rope_tables",
    )(pos_col, mla_f, dil_f, sign)


def _norm_kernel(x_ref, g_ref, sc_ref, sh_ref, o_ref):
    x = x_ref[...]
    y = x * lax.rsqrt(jnp.mean(x * x, axis=-1, keepdims=True) + NORM_EPS) * g_ref[...]
    o_ref[...] = (y * (1.0 + sc_ref[...]) + sh_ref[...]).astype(o_ref.dtype)


def _norm_call(x2, g_pre, scale3, shift3, s_len, tm=512):
    m, d = x2.shape
    per_b = s_len // tm
    vec = pl.BlockSpec((None, 1, d), lambda i: (i // per_b, 0, 0))
    return pl.pallas_call(
        _norm_kernel,
        grid=(m // tm,),
        in_specs=[pl.BlockSpec((tm, d), lambda i: (i, 0)),
                  pl.BlockSpec((1, d), lambda i: (0, 0)), vec, vec],
        out_specs=pl.BlockSpec((tm, d), lambda i: (i, 0)),
        out_shape=jax.ShapeDtypeStruct((m, d), BF16),
        compiler_params=_cparams(("arbitrary",)),
        name="norm_mod",
    )(x2, g_pre, scale3, shift3)


def _rotate_groups(acc, c_ref, s_ref, o_ref, scale):
    cos = c_ref[...]
    sin = s_ref[...]
    for g in range(acc.shape[1] // LANES):
        xg = acc[:, g * LANES:(g + 1) * LANES]
        r = xg * cos + pltpu.roll(xg, HALF, 1) * sin
        if scale != 1.0:
            r = r * scale
        o_ref[:, g * LANES:(g + 1) * LANES] = r.astype(o_ref.dtype)


def _sigmoid(z):
    return 1.0 / (1.0 + jnp.exp(-z))


def _proj_kernel(*refs, mode, tn):
    if mode in ("q", "dil"):
        h_ref, w_ref, c_ref, s_ref, o_ref = refs
    else:
        h_ref, w_ref, o_ref = refs
    j = pl.program_id(1)
    acc = jnp.dot(h_ref[...], w_ref[...], preferred_element_type=F32)

    if mode == "kv":
        o_ref[...] = acc
    elif mode == "q":
        scale = (MLA_QK ** -0.5) * LOG2E
        n_nope = MLA_HEADS * MLA_NOPE // tn

        @pl.when(j < n_nope)
        def _():
            o_ref[...] = (acc * scale).astype(o_ref.dtype)

        @pl.when(j >= n_nope)
        def _():
            _rotate_groups(acc, c_ref, s_ref, o_ref, scale)
    elif mode == "gates":
        n_z_mla = MLA_HEADS * MLA_V // tn
        n_sig = 2 * D_MODEL // tn
        sg = _sigmoid(acc)

        @pl.when((j < n_z_mla) | (j >= n_z_mla + n_sig))
        def _():
            o_ref[...] = (acc * sg).astype(o_ref.dtype)

        @pl.when((j >= n_z_mla) & (j < n_z_mla + n_sig))
        def _():
            o_ref[...] = sg.astype(o_ref.dtype)
    elif mode == "dil":
        scale = (DIL_HEAD_DIM ** -0.5) * LOG2E
        per = DIL_N_GROUPS * DIL_WIDTH // tn

        @pl.when(j < per)
        def _():
            _rotate_groups(acc, c_ref, s_ref, o_ref, scale)

        @pl.when((j >= per) & (j < 2 * per))
        def _():
            _rotate_groups(acc, c_ref, s_ref, o_ref, 1.0)

        @pl.when(j >= 2 * per)
        def _():
            o_ref[...] = acc.astype(o_ref.dtype)


def _proj_call(h, w, mode, tn, out_dtype, tables=None, tm=1024):
    m, k = h.shape
    n = w.shape[1]
    in_specs = [pl.BlockSpec((tm, k), lambda i, j: (i, 0)),
                pl.BlockSpec((k, tn), lambda i, j: (0, j))]
    args = [h, w]
    if tables is not None:
        tab = pl.BlockSpec((tm, LANES), lambda i, j: (i, 0))
        in_specs += [tab, tab]
        args += list(tables)
    return pl.pallas_call(
        functools.partial(_proj_kernel, mode=mode, tn=tn),
        grid=(m // tm, n // tn),
        in_specs=in_specs,
        out_specs=pl.BlockSpec((tm, tn), lambda i, j: (i, j)),
        out_shape=jax.ShapeDtypeStruct((m, n), out_dtype),
        compiler_params=_cparams(("arbitrary", "arbitrary")),
        name="proj_" + mode,
    )(*args)


def _kvb_kernel(kv_ref, g_ref, w_ref, c_ref, s_ref, k_ref, v_ref):
    kv = kv_ref[...]
    lat = kv[:, :MLA_KV_RANK]
    lat = lat * lax.rsqrt(jnp.mean(lat * lat, axis=-1, keepdims=True) + NORM_EPS) * g_ref[...]
    kvb = jnp.dot(lat.astype(BF16), w_ref[...], preferred_element_type=F32)
    pe = kv[:, MLA_KV_RANK:]
    pe = pe * c_ref[...] + pltpu.roll(pe, HALF, 1) * s_ref[...]
    lane = lax.broadcasted_iota(jnp.int32, pe.shape, 1)
    first = (lane % HALF) < (HALF // 2)
    pe_even = jnp.where(first, pe, 0.0).astype(k_ref.dtype)
    pe_odd = jnp.where(first, 0.0, pe).astype(k_ref.dtype)
    for h in range(MLA_HEADS):
        k_ref[:, h * MLA_KPAD:h * MLA_KPAD + MLA_NOPE] = (
            kvb[:, h * MLA_NOPE:(h + 1) * MLA_NOPE].astype(k_ref.dtype))
        k_ref[:, h * MLA_KPAD + MLA_NOPE:(h + 1) * MLA_KPAD] = pe_even if h % 2 == 0 else pe_odd
    v_ref[...] = kvb[:, MLA_HEADS * MLA_NOPE:].astype(v_ref.dtype)


def _kvb_call(kv_a, g_kv, w_kvb, tables, tm=256):
    m, kw = kv_a.shape
    nk = MLA_HEADS * MLA_KPAD
    nv = MLA_HEADS * MLA_V
    tab = pl.BlockSpec((tm, LANES), lambda i: (i, 0))
    return pl.pallas_call(
        _kvb_kernel,
        grid=(m // tm,),
        in_specs=[pl.BlockSpec((tm, kw), lambda i: (i, 0)),
                  pl.BlockSpec((1, MLA_KV_RANK), lambda i: (0, 0)),
                  pl.BlockSpec(w_kvb.shape, lambda i: (0, 0)), tab, tab],
        out_specs=[pl.BlockSpec((tm, nk), lambda i: (i, 0)),
                   pl.BlockSpec((tm, nv), lambda i: (i, 0))],
        out_shape=[jax.ShapeDtypeStruct((m, nk), BF16), jax.ShapeDtypeStruct((m, nv), BF16)],
        compiler_params=_cparams(("arbitrary",)),
        name="kvb",
    )(kv_a, g_kv, w_kvb, *tables)


def _mla_kernel(qn_ref, qp_ref, k_ref, v_ref, z_ref, o_ref, *, tq, tk):
    i = pl.program_id(2)
    q = jnp.concatenate([qn_ref[...], qp_ref[...]], axis=1)

    def step(off, carry, mask):
        m, l, acc = carry
        kj = k_ref[pl.ds(off, tk), :]
        vj = v_ref[pl.ds(off, tk), :]
        s = lax.dot_general(q, kj, (((1,), (1,)), ((), ())), preferred_element_type=F32)
        if mask is not None:
            s = jnp.where(mask, s, NEG)
        m_new = jnp.maximum(m, jnp.max(s, axis=1, keepdims=True))
        p = jnp.exp2(s - m_new)
        alpha = jnp.exp2(m - m_new)
        l = alpha * l + jnp.sum(p, axis=1, keepdims=True)
        acc = alpha * acc + jnp.dot(p.astype(BF16), vj, preferred_element_type=F32)
        return m_new, l, acc

    carry = (jnp.full((tq, 1), NEG, F32), jnp.zeros((tq, 1), F32), jnp.zeros((tq, MLA_V), F32))
    n_sub = tq // tk

    def body(j, c):
        return step(pl.multiple_of(j * tk, tk), c, None)

    carry = lax.fori_loop(0, i * n_sub, body, carry)

    rows = lax.broadcasted_iota(jnp.int32, (tq, tk), 0)
    cols = lax.broadcasted_iota(jnp.int32, (tq, tk), 1)
    for d in range(n_sub):
        off = pl.multiple_of(i * tq + d * tk, tk)
        carry = step(off, carry, cols + d * tk <= rows)

    m, l, acc = carry
    o_ref[...] = (acc / l * z_ref[...].astype(F32)).astype(o_ref.dtype)


def _mla_call(q, k, v, gates, tq=1024, tk=512):
    b, s, _ = q.shape
    pe_blk0 = MLA_HEADS * MLA_NOPE // LANES
    return pl.pallas_call(
        functools.partial(_mla_kernel, tq=tq, tk=tk),
        grid=(b, MLA_HEADS, s // tq),
        in_specs=[pl.BlockSpec((None, tq, MLA_NOPE), lambda bb, h, i: (bb, i, h)),
                  pl.BlockSpec((None, tq, LANES), lambda bb, h, i: (bb, i, pe_blk0 + h // 2)),
                  pl.BlockSpec((None, s, MLA_KPAD), lambda bb, h, i: (bb, 0, h)),
                  pl.BlockSpec((None, s, MLA_V), lambda bb, h, i: (bb, 0, h)),
                  pl.BlockSpec((None, tq, MLA_V), lambda bb, h, i: (bb, i, h))],
        out_specs=pl.BlockSpec((None, tq, MLA_V), lambda bb, h, i: (bb, i, h)),
        out_shape=jax.ShapeDtypeStruct((b, s, MLA_HEADS * MLA_V), BF16),
        compiler_params=_cparams(("arbitrary", "arbitrary", "arbitrary")),
        name="mla_attn",
    )(q, q, k, v, gates)


def _dil_kernel(q_ref, kp_ref, kc_ref, vp_ref, vc_ref, o_ref, lse_ref):
    i = pl.program_id(2)
    qi = lax.broadcasted_iota(jnp.int32, (DIL_BLOCK, 2 * DIL_BLOCK), 0)
    kj = lax.broadcasted_iota(jnp.int32, (DIL_BLOCK, 2 * DIL_BLOCK), 1)
    valid = (kj >= qi) & (kj <= qi + DIL_BLOCK) & ((i > 0) | (kj >= DIL_BLOCK))
    lane = lax.broadcasted_iota(jnp.int32, (DIL_BLOCK, LANES), 1)
    stats = jnp.zeros((DIL_BLOCK, LANES), F32)
    for h in range(DIL_HEADS):
        hs = slice(h * DIL_HEAD_DIM, (h + 1) * DIL_HEAD_DIM)
        q = q_ref[:, hs]
        kk = jnp.concatenate([kp_ref[:, hs], kc_ref[:, hs]], axis=0)
        vv = jnp.concatenate([vp_ref[:, hs], vc_ref[:, hs]], axis=0)
        s = lax.dot_general(q, kk, (((1,), (1,)), ((), ())), preferred_element_type=F32)
        s = jnp.where(valid, s, NEG)
        m = jnp.max(s, axis=1, keepdims=True)
        p = jnp.exp2(s - m)
        l = jnp.sum(p, axis=1, keepdims=True)
        o = jnp.dot(p.astype(BF16), vv, preferred_element_type=F32) / l
        o_ref[:, hs] = o.astype(o_ref.dtype)
        stats = jnp.where(lane == h, m + jnp.log2(l), stats)
    lse_ref[...] = stats


def _dil_call(qkv, g, dilation):
    b, s, n = qkv.shape
    sd = s // dilation
    per = n // DIL_WIDTH
    view = qkv.reshape(b, sd, dilation * n)
    blk = lambda col0, prev: pl.BlockSpec(
        (None, DIL_BLOCK, DIL_WIDTH),
        (lambda bb, r, i: (bb, jnp.maximum(i - 1, 0), r * per + col0)) if prev
        else (lambda bb, r, i: (bb, i, r * per + col0)))
    o, lse = pl.pallas_call(
        _dil_kernel,
        grid=(b, dilation, sd // DIL_BLOCK),
        in_specs=[blk(g, False), blk(DIL_N_GROUPS + g, True), blk(DIL_N_GROUPS + g, False),
                  blk(2 * DIL_N_GROUPS + g, True), blk(2 * DIL_N_GROUPS + g, False)],
        out_specs=[pl.BlockSpec((None, DIL_BLOCK, DIL_WIDTH), lambda bb, r, i: (bb, i, r)),
                   pl.BlockSpec((None, DIL_BLOCK, LANES), lambda bb, r, i: (bb, i, r))],
        out_shape=[jax.ShapeDtypeStruct((b, sd, dilation * DIL_WIDTH), BF16),
                   jax.ShapeDtypeStruct((b, sd, dilation * LANES), F32)],
        compiler_params=_cparams(("arbitrary", "arbitrary", "arbitrary")),
        name="dil_attn_g%d" % g,
    )(view, view, view, view, view)
    return o.reshape(b * s, DIL_WIDTH), lse.reshape(b * s, LANES)


def _merge_kernel(a_ref, o0_ref, o1_ref, o2_ref, l0_ref, l1_ref, l2_ref, zd_ref, gm_ref, gd_ref,
                  wm_ref, wd_ref, out_ref):
    l0, l1, l2 = l0_ref[...], l1_ref[...], l2_ref[...]
    mx = jnp.maximum(jnp.maximum(l0, l1), l2)
    e0, e1, e2 = jnp.exp2(l0 - mx), jnp.exp2(l1 - mx), jnp.exp2(l2 - mx)
    den = e0 + e1 + e2
    w0, w1, w2 = e0 / den, e1 / den, e2 / den
    parts = []
    for h in range(DIL_HEADS):
        hs = slice(h * DIL_HEAD_DIM, (h + 1) * DIL_HEAD_DIM)
        comb = (w0[:, h:h + 1] * o0_ref[:, hs].astype(F32)
                + w1[:, h:h + 1] * o1_ref[:, hs].astype(F32)
                + w2[:, h:h + 1] * o2_ref[:, hs].astype(F32))
        parts.append((comb * zd_ref[:, hs].astype(F32)).astype(BF16))
    bb = jnp.concatenate(parts, axis=1)
    y_mla = jnp.dot(a_ref[...], wm_ref[...], preferred_element_type=F32)
    y_dil = jnp.dot(bb, wd_ref[...], preferred_element_type=F32)
    merged = gm_ref[...].astype(F32) * y_mla + gd_ref[...].astype(F32) * y_dil
    out_ref[...] = merged.astype(out_ref.dtype)


def _merge_call(a, outs, lses, gates, w_mla_proj, w_dil_proj, tm=512):
    m = a.shape[0]
    d = D_MODEL
    row = lambda w, c: pl.BlockSpec((tm, w), lambda i, c=c: (i, c))
    const = lambda shp: pl.BlockSpec(shp, lambda i: (0, 0))
    zd_blk = 3 * d // DIL_WIDTH
    return pl.pallas_call(
        _merge_kernel,
        grid=(m // tm,),
        in_specs=[row(MLA_HEADS * MLA_V, 0),
                  row(DIL_WIDTH, 0), row(DIL_WIDTH, 0), row(DIL_WIDTH, 0),
                  row(LANES, 0), row(LANES, 0), row(LANES, 0),
                  row(DIL_WIDTH, zd_blk), row(d, 1), row(d, 2),
                  const(w_mla_proj.shape), const(w_dil_proj.shape)],
        out_specs=pl.BlockSpec((tm, d), lambda i: (i, 0)),
        out_shape=jax.ShapeDtypeStruct((m, d), BF16),
        compiler_params=_cparams(("arbitrary",)),
        name="merge",
    )(a, *outs, *lses, gates, gates, gates, w_mla_proj, w_dil_proj)


def _final_kernel(mg_ref, w_ref, x_ref, gate_ref, g_ref, o_ref):
    out = jnp.dot(mg_ref[...], w_ref[...], preferred_element_type=F32)
    y = out * lax.rsqrt(jnp.mean(out * out, axis=-1, keepdims=True) + NORM_EPS) * g_ref[...]
    o_ref[...] = x_ref[...] + gate_ref[...] * y


def _final_call(merged, w_o, x2, gate3, g_post, s_len, tm=512):
    m, d = x2.shape
    per_b = s_len // tm
    return pl.pallas_call(
        _final_kernel,
        grid=(m // tm,),
        in_specs=[pl.BlockSpec((tm, d), lambda i: (i, 0)),
                  pl.BlockSpec(w_o.shape, lambda i: (0, 0)),
                  pl.BlockSpec((tm, d), lambda i: (i, 0)),
                  pl.BlockSpec((None, 1, d), lambda i: (i // per_b, 0, 0)),
                  pl.BlockSpec((1, d), lambda i: (0, 0))],
        out_specs=pl.BlockSpec((tm, d), lambda i: (i, 0)),
        out_shape=jax.ShapeDtypeStruct((m, d), F32),
        compiler_params=_cparams(("arbitrary",)),
        name="final",
    )(merged, w_o, x2, gate3, g_post)


def kernel(x, c, positions, w_ada, b_ada, g_pre, g_post, w_in, g_kv, w_kv_b, w_mla_proj, w_dil_proj, w_o):
    depth = w_in.shape[0]
    b, s, d = x.shape
    assert d == D_MODEL and s % (DIL_GROUPS[-1][1] * DIL_BLOCK) == 0
    m = b * s
    q_cols, kv_cols, g_cols, d_cols = _in_proj_columns()
    pad_rows = 8
    c_pad = jnp.zeros((pad_rows, d), F32).at[:b].set(c)
    tables = _tables_call(positions.reshape(m, 1))
    mla_tabs, dil_tabs = tables[:2], tables[2:]

    for layer in range(depth):
        mod = _ada_call(c_pad, w_ada[layer], b_ada[layer][None, :])[:b]
        shift, scale, gate = (mod[:, None, k * d:(k + 1) * d] for k in range(3))

        w = w_in[layer]
        w_q = w[:, q_cols].astype(BF16)
        w_kv = w[:, kv_cols].astype(BF16)
        w_g = w[:, g_cols].astype(BF16)
        w_d = w[:, d_cols].astype(BF16)
        wkb = w_kv_b[layer].reshape(MLA_KV_RANK, MLA_HEADS, MLA_NOPE + MLA_V)
        wkb = jnp.concatenate([wkb[:, :, :MLA_NOPE].reshape(MLA_KV_RANK, -1),
                               wkb[:, :, MLA_NOPE:].reshape(MLA_KV_RANK, -1)], axis=1).astype(BF16)

        x2 = x.reshape(m, d)
        h = _norm_call(x2, g_pre[layer][None, :], scale, shift, s)

        q = _proj_call(h, w_q, "q", 1024, BF16, mla_tabs)
        kv_a = _proj_call(h, w_kv, "kv", w_kv.shape[1], F32)
        gates = _proj_call(h, w_g, "gates", 1024, BF16)
        qkv = _proj_call(h, w_d, "dil", 1024, BF16, dil_tabs)

        k_mla, v_mla = _kvb_call(kv_a, g_kv[layer][None, :], wkb, mla_tabs)
        a = _mla_call(q.reshape(b, s, -1), k_mla.reshape(b, s, -1), v_mla.reshape(b, s, -1),
                      gates.reshape(b, s, -1))

        qkv3 = qkv.reshape(b, s, -1)
        outs, lses = [], []
        for g, (_, dilation) in enumerate(DIL_GROUPS):
            o_g, lse_g = _dil_call(qkv3, g, dilation)
            outs.append(o_g)
            lses.append(lse_g)

        merged = _merge_call(a.reshape(m, -1), outs, lses, gates,
                             w_mla_proj[layer].astype(BF16), w_dil_proj[layer].astype(BF16))
        x = _final_call(merged, w_o[layer].astype(BF16), x2, gate, g_post[layer][None, :], s).reshape(b, s, d)
    return x
```

```python
import functools
import math

import numpy as np
import jax
import jax.numpy as jnp
from jax import lax
from jax.experimental import pallas as pl
from jax.experimental.pallas import tpu as pltpu

D_MODEL = 2048
ROPE_THETA = 500000.0
NORM_EPS = 1e-6

MLA_HEADS = 16
MLA_NOPE = 128
MLA_ROPE = 64
MLA_V = 128
MLA_QK = MLA_NOPE + MLA_ROPE
MLA_KV_RANK = 512
MLA_KPAD = 256

DIL_GROUPS = ((128, 1), (512, 4), (2048, 16))
DIL_N_GROUPS = 3
DIL_HEADS = 8
DIL_HEAD_DIM = 128
DIL_ROT = 32
DIL_WIDTH = DIL_HEADS * DIL_HEAD_DIM
DIL_BLOCK = 128

LANES = 128
HALF = LANES // 2
NEG = -1e30
LOG2E = math.log2(math.e)
VMEM_LIMIT = 56 * 1024 * 1024

F32 = jnp.float32
BF16 = jnp.bfloat16


def _cparams(sem, flags=None):
    return pltpu.CompilerParams(dimension_semantics=sem, vmem_limit_bytes=VMEM_LIMIT, flags=flags)


def _in_proj_columns():
    q0 = 0
    kv0 = MLA_HEADS * MLA_QK
    z_mla0 = kv0 + MLA_KV_RANK + MLA_ROPE
    dil0 = z_mla0 + MLA_HEADS * MLA_V
    z_dil0 = dil0 + 3 * DIL_N_GROUPS * DIL_WIDTH
    gl_mla0 = z_dil0 + DIL_WIDTH
    gl_dil0 = gl_mla0 + D_MODEL
    half = MLA_ROPE // 2

    q_cols = []
    for h in range(MLA_HEADS):
        q_cols.append(q0 + h * MLA_QK + np.arange(MLA_NOPE))
    for j in range(MLA_HEADS // 2):
        a, b = 2 * j, 2 * j + 1
        for part in (0, 1):
            for hh in (a, b):
                q_cols.append(q0 + hh * MLA_QK + MLA_NOPE + part * half + np.arange(half))
    q_cols = np.concatenate(q_cols)

    kx1 = kv0 + MLA_KV_RANK + np.arange(half)
    kx2 = kx1 + half
    kv_cols = np.concatenate([kv0 + np.arange(MLA_KV_RANK), kx1, kx1, kx2, kx2])

    g_cols = np.concatenate([z_mla0 + np.arange(MLA_HEADS * MLA_V), gl_mla0 + np.arange(D_MODEL),
                             gl_dil0 + np.arange(D_MODEL), z_dil0 + np.arange(DIL_WIDTH)])

    rh = DIL_ROT // 2
    perm = np.concatenate([np.arange(0, rh), np.arange(DIL_ROT, DIL_ROT + HALF - rh),
                           np.arange(rh, DIL_ROT), np.arange(DIL_ROT + HALF - rh, DIL_HEAD_DIM)])
    n_heads = DIL_N_GROUPS * DIL_HEADS
    d_cols = []
    for g in range(DIL_N_GROUPS):
        cols = []
        for part in range(3):
            for hh in range(g * DIL_HEADS, (g + 1) * DIL_HEADS):
                base = dil0 + (part * n_heads + hh) * DIL_HEAD_DIM
                cols.append(base + (perm if part < 2 else np.arange(DIL_HEAD_DIM)))
        d_cols.append(np.concatenate(cols))
    return q_cols, kv_cols, g_cols, d_cols


def _take_columns(w, cols):
    breaks = np.flatnonzero(np.diff(cols) != 1) + 1
    starts = np.concatenate([[0], breaks])
    ends = np.concatenate([breaks, [len(cols)]])
    return jnp.concatenate([w[:, int(cols[a]):int(cols[a]) + int(e - a)] for a, e in zip(starts, ends)],
                           axis=1)


def _rope_lane_rows():
    inv64 = 1.0 / (ROPE_THETA ** (jnp.arange(0, MLA_ROPE, 2, dtype=F32) / MLA_ROPE))
    inv32 = 1.0 / (ROPE_THETA ** (jnp.arange(0, DIL_ROT, 2, dtype=F32) / DIL_ROT))
    mla_f = jnp.tile(inv64, 4)[None, :]
    z = jnp.zeros((HALF - DIL_ROT // 2,), F32)
    dil_f = jnp.concatenate([inv32, z, inv32, z])[None, :]
    sign = jnp.concatenate([-jnp.ones((HALF,), F32), jnp.ones((HALF,), F32)])[None, :]
    return mla_f, dil_f, sign


def _ada_kernel(c_ref, w_ref, b_ref, o_ref):
    c = c_ref[...]
    sc = c * (1.0 / (1.0 + jnp.exp(-c)))
    o_ref[...] = jnp.dot(sc, w_ref[...], preferred_element_type=F32) + b_ref[...]


def _ada_call(c_pad, w_ada, b_ada, tn=512):
    rows, d = c_pad.shape
    n = w_ada.shape[1]
    return pl.pallas_call(
        _ada_kernel,
        grid=(n // tn,),
        in_specs=[pl.BlockSpec((rows, d), lambda j: (0, 0)),
                  pl.BlockSpec((d, tn), lambda j: (0, j)),
                  pl.BlockSpec((1, tn), lambda j: (0, j))],
        out_specs=pl.BlockSpec((rows, tn), lambda j: (0, j)),
        out_shape=jax.ShapeDtypeStruct((rows, n), F32),
        compiler_params=_cparams(("arbitrary",)),
        name="ada",
    )(c_pad, w_ada, b_ada)


def _tables_kernel(pos_ref, mf_ref, df_ref, sg_ref, mc_ref, ms_ref, dc_ref, ds_ref):
    pos = pos_ref[...].astype(F32)
    sg = sg_ref[...]
    ang = pos * mf_ref[...]
    mc_ref[...] = jnp.cos(ang)
    ms_ref[...] = jnp.sin(ang) * sg
    ang = pos * df_ref[...]
    dc_ref[...] = jnp.cos(ang)
    ds_ref[...] = jnp.sin(ang) * sg


def _tables_call(pos_col, tm=1024):
    m = pos_col.shape[0]
    mla_f, dil_f, sign = _rope_lane_rows()
    row = pl.BlockSpec((1, LANES), lambda i: (0, 0))
    tab = pl.BlockSpec((tm, LANES), lambda i: (i, 0))
    shp = jax.ShapeDtypeStruct((m, LANES), F32)
    return pl.pallas_call(
        _tables_kernel,
        grid=(m // tm,),
        in_specs=[pl.BlockSpec((tm, 1), lambda i: (i, 0)), row, row, row],
        out_specs=[tab, tab, tab, tab],
        out_shape=[shp, shp, shp, shp],
        compiler_params=_cparams(("arbitrary",)),
        name="rope_tables",
    )(pos_col, mla_f, dil_f, sign)


def _norm_kernel(x_ref, g_ref, sc_ref, sh_ref, o_ref):
    x = x_ref[...]
    y = x * lax.rsqrt(jnp.mean(x * x, axis=-1, keepdims=True) + NORM_EPS) * g_ref[...]
    o_ref[...] = (y * (1.0 + sc_ref[...]) + sh_ref[...]).astype(o_ref.dtype)


def _norm_call(x2, g_pre, scale3, shift3, s_len, tm=512):
    m, d = x2.shape
    per_b = s_len // tm
    vec = pl.BlockSpec((None, 1, d), lambda i: (i // per_b, 0, 0))
    return pl.pallas_call(
        _norm_kernel,
        grid=(m // tm,),
        in_specs=[pl.BlockSpec((tm, d), lambda i: (i, 0)),
                  pl.BlockSpec((1, d), lambda i: (0, 0)), vec, vec],
        out_specs=pl.BlockSpec((tm, d), lambda i: (i, 0)),
        out_shape=jax.ShapeDtypeStruct((m, d), BF16),
        compiler_params=_cparams(("arbitrary",)),
        name="norm_mod",
    )(x2, g_pre, scale3, shift3)


def _rotate_groups(acc, c_ref, s_ref, o_ref, scale):
    cos = c_ref[...]
    sin = s_ref[...]
    for g in range(acc.shape[1] // LANES):
        xg = acc[:, g * LANES:(g + 1) * LANES]
        r = xg * cos + pltpu.roll(xg, HALF, 1) * sin
        if scale != 1.0:
            r = r * scale
        o_ref[:, g * LANES:(g + 1) * LANES] = r.astype(o_ref.dtype)


def _sigmoid(z):
    return 1.0 / (1.0 + jnp.exp(-z))


def _proj_kernel(*refs, mode, tn):
    if mode == "q":
        h_ref, w_ref, c_ref, s_ref, o_ref = refs
    else:
        h_ref, w_ref, o_ref = refs
    j = pl.program_id(1)
    acc = jnp.dot(h_ref[...], w_ref[...], preferred_element_type=F32)

    if mode == "kv":
        o_ref[...] = acc
    elif mode == "q":
        scale = (MLA_QK ** -0.5) * LOG2E
        n_nope = MLA_HEADS * MLA_NOPE // tn

        @pl.when(j < n_nope)
        def _():
            o_ref[...] = (acc * scale).astype(o_ref.dtype)

        @pl.when(j >= n_nope)
        def _():
            _rotate_groups(acc, c_ref, s_ref, o_ref, scale)
    elif mode == "gates":
        n_z_mla = MLA_HEADS * MLA_V // tn
        n_sig = 2 * D_MODEL // tn
        sg = _sigmoid(acc)

        @pl.when((j < n_z_mla) | (j >= n_z_mla + n_sig))
        def _():
            o_ref[...] = (acc * sg).astype(o_ref.dtype)

        @pl.when((j >= n_z_mla) & (j < n_z_mla + n_sig))
        def _():
            o_ref[...] = sg.astype(o_ref.dtype)


def _dilproj_kernel(h_ref, w_ref, c_ref, s_ref, o_ref, *scratch, dilation):
    j = pl.program_id(2)
    tm = h_ref.shape[0]
    acc = jnp.dot(h_ref[...], w_ref[...], preferred_element_type=F32)
    n_groups = acc.shape[1] // LANES
    scale = (DIL_HEAD_DIM ** -0.5) * LOG2E

    def emit(g, val):
        if dilation == 1:
            o_ref[0, :, g * LANES:(g + 1) * LANES] = val.astype(o_ref.dtype)
        else:
            scratch[0][g] = val

    def rotated(mult):
        cos = c_ref[...]
        sin = s_ref[...]
        for g in range(n_groups):
            xg = acc[:, g * LANES:(g + 1) * LANES]
            r = xg * cos + pltpu.roll(xg, HALF, 1) * sin
            emit(g, r * mult if mult != 1.0 else r)

    @pl.when(j == 0)
    def _():
        rotated(scale)

    @pl.when(j == 1)
    def _():
        rotated(1.0)

    @pl.when(j == 2)
    def _():
        for g in range(n_groups):
            emit(g, acc[:, g * LANES:(g + 1) * LANES])

    if dilation > 1:
        rows = tm // dilation
        for r in range(dilation):
            for g in range(n_groups):
                o_ref[r, :, g * LANES:(g + 1) * LANES] = (
                    scratch[0][g, pl.ds(r, rows, stride=dilation), :].astype(o_ref.dtype))


def _dilproj_call(h3, w, tables3, dilation, tm=1024):
    b, s, k = h3.shape
    n = w.shape[1]
    tn = DIL_WIDTH
    scratch = [] if dilation == 1 else [pltpu.VMEM((tn // LANES, tm, LANES), F32)]
    tab = pl.BlockSpec((None, tm, LANES), lambda bb, i, j: (bb, i, 0))
    return pl.pallas_call(
        functools.partial(_dilproj_kernel, dilation=dilation),
        grid=(b, s // tm, n // tn),
        in_specs=[pl.BlockSpec((None, tm, k), lambda bb, i, j: (bb, i, 0)),
                  pl.BlockSpec((k, tn), lambda bb, i, j: (0, j)), tab, tab],
        out_specs=pl.BlockSpec((None, dilation, tm // dilation, tn), lambda bb, i, j: (bb, 0, i, j)),
        out_shape=jax.ShapeDtypeStruct((b, dilation, s // dilation, n), BF16),
        scratch_shapes=scratch,
        compiler_params=_cparams(("arbitrary", "arbitrary", "arbitrary")),
        name="proj_dil_d%d" % dilation,
    )(h3, w, *tables3)


def _proj_call(h, w, mode, tn, out_dtype, tables=None, tm=1024):
    m, k = h.shape
    n = w.shape[1]
    in_specs = [pl.BlockSpec((tm, k), lambda i, j: (i, 0)),
                pl.BlockSpec((k, tn), lambda i, j: (0, j))]
    args = [h, w]
    if tables is not None:
        tab = pl.BlockSpec((tm, LANES), lambda i, j: (i, 0))
        in_specs += [tab, tab]
        args += list(tables)
    return pl.pallas_call(
        functools.partial(_proj_kernel, mode=mode, tn=tn),
        grid=(m // tm, n // tn),
        in_specs=in_specs,
        out_specs=pl.BlockSpec((tm, tn), lambda i, j: (i, j)),
        out_shape=jax.ShapeDtypeStruct((m, n), out_dtype),
        compiler_params=_cparams(("arbitrary", "arbitrary")),
        name="proj_" + mode,
    )(*args)


def _kvb_kernel(kv_ref, g_ref, w_ref, c_ref, s_ref, k_ref, v_ref):
    kv = kv_ref[...]
    lat = kv[:, :MLA_KV_RANK]
    lat = lat * lax.rsqrt(jnp.mean(lat * lat, axis=-1, keepdims=True) + NORM_EPS) * g_ref[...]
    kvb = jnp.dot(lat.astype(BF16), w_ref[...], preferred_element_type=F32)
    pe = kv[:, MLA_KV_RANK:]
    pe = pe * c_ref[...] + pltpu.roll(pe, HALF, 1) * s_ref[...]
    lane = lax.broadcasted_iota(jnp.int32, pe.shape, 1)
    first = (lane % HALF) < (HALF // 2)
    pe_even = jnp.where(first, pe, 0.0).astype(k_ref.dtype)
    pe_odd = jnp.where(first, 0.0, pe).astype(k_ref.dtype)
    for h in range(MLA_HEADS):
        k_ref[:, h * MLA_KPAD:h * MLA_KPAD + MLA_NOPE] = (
            kvb[:, h * MLA_NOPE:(h + 1) * MLA_NOPE].astype(k_ref.dtype))
        k_ref[:, h * MLA_KPAD + MLA_NOPE:(h + 1) * MLA_KPAD] = pe_even if h % 2 == 0 else pe_odd
    v_ref[...] = kvb[:, MLA_HEADS * MLA_NOPE:].astype(v_ref.dtype)


def _kvb_call(kv_a, g_kv, w_kvb, tables, tm=256):
    m, kw = kv_a.shape
    nk = MLA_HEADS * MLA_KPAD
    nv = MLA_HEADS * MLA_V
    tab = pl.BlockSpec((tm, LANES), lambda i: (i, 0))
    return pl.pallas_call(
        _kvb_kernel,
        grid=(m // tm,),
        in_specs=[pl.BlockSpec((tm, kw), lambda i: (i, 0)),
                  pl.BlockSpec((1, MLA_KV_RANK), lambda i: (0, 0)),
                  pl.BlockSpec(w_kvb.shape, lambda i: (0, 0)), tab, tab],
        out_specs=[pl.BlockSpec((tm, nk), lambda i: (i, 0)),
                   pl.BlockSpec((tm, nv), lambda i: (i, 0))],
        out_shape=[jax.ShapeDtypeStruct((m, nk), BF16), jax.ShapeDtypeStruct((m, nv), BF16)],
        compiler_params=_cparams(("arbitrary",)),
        name="kvb",
    )(kv_a, g_kv, w_kvb, *tables)


def _mla_kernel(qn_ref, qp_ref, k_ref, v_ref, z_ref, o_ref,
                s0_ref, s1_ref, p0_ref, p1_ref, a0_ref, a1_ref, acc_ref, m_ref, l_ref, *, tq, tk, rb):
    i = pl.program_id(2)
    q = jnp.concatenate([qn_ref[...], qp_ref[...]], axis=1)
    s_bufs = (s0_ref, s1_ref)
    p_bufs = (p0_ref, p1_ref)
    a_bufs = (a0_ref, a1_ref)
    nt = (((1,), (1,)), ((), ()))

    def chunk(ref, off):
        return ref[pl.ds(off if isinstance(off, int) else pl.multiple_of(off, tk), tk), :]

    def scores(off, slot, row0=0):
        s_bufs[slot][row0:, :] = lax.dot_general(q[row0:], chunk(k_ref, off), nt,
                                                 preferred_element_type=F32)

    def softmax(slot, row0=0, col0=None):
        for r0 in range(row0, tq, rb):
            rs = slice(r0, r0 + rb)
            s = s_bufs[slot][rs, :]
            if col0 is not None and col0 + tk - 1 > r0:
                rows = r0 + lax.broadcasted_iota(jnp.int32, (rb, tk), 0)
                cols = col0 + lax.broadcasted_iota(jnp.int32, (rb, tk), 1)
                s = jnp.where(cols <= rows, s, NEG)
            m_old = m_ref[rs, :]
            m_new = jnp.maximum(m_old, jnp.max(s, axis=1, keepdims=True))
            psum = None
            for c in range(tk // LANES):
                cs = slice(c * LANES, (c + 1) * LANES)
                p = jnp.exp2(s[:, cs] - m_new)
                psum = p if psum is None else psum + p
                p_bufs[slot][rs, cs] = p.astype(BF16)
            alpha = jnp.exp2(m_old - m_new)
            l_ref[rs, :] = alpha * l_ref[rs, :] + psum
            m_ref[rs, :] = m_new
            a_bufs[slot][rs, :] = alpha

    def pv(off, slot, row0=0):
        upd = jnp.dot(p_bufs[slot][row0:, :], chunk(v_ref, off), preferred_element_type=F32)
        acc_ref[row0:, :] = a_bufs[slot][row0:, :] * acc_ref[row0:, :] + upd

    m_ref[...] = jnp.full(m_ref.shape, NEG, F32)
    l_ref[...] = jnp.zeros(l_ref.shape, F32)
    acc_ref[...] = jnp.zeros(acc_ref.shape, F32)

    scores(0, 0)

    @pl.loop(0, i)
    def _(t):
        base = pl.multiple_of(t * tq, tq)
        scores(base + tk, 1)
        softmax(0)
        pv(base, 0)
        scores(base + tq, 0)
        softmax(1)
        pv(base + tk, 1)

    base = pl.multiple_of(i * tq, tq)
    scores(base + tk, 1, row0=tk)
    softmax(0, col0=0)
    pv(base, 0)
    softmax(1, row0=tk, col0=tk)
    pv(base + tk, 1, row0=tk)

    l = jnp.sum(l_ref[...], axis=1, keepdims=True)
    o_ref[...] = (acc_ref[...] / l * z_ref[...].astype(F32)).astype(o_ref.dtype)


def _mla_call(q, k, v, gates, tq=1024, tk=512, rb=32):
    b, s, _ = q.shape
    assert tq == 2 * tk and s % tq == 0
    pe_blk0 = MLA_HEADS * MLA_NOPE // LANES
    return pl.pallas_call(
        functools.partial(_mla_kernel, tq=tq, tk=tk, rb=rb),
        scratch_shapes=[pltpu.VMEM((tq, tk), F32), pltpu.VMEM((tq, tk), F32),
                        pltpu.VMEM((tq, tk), BF16), pltpu.VMEM((tq, tk), BF16),
                        pltpu.VMEM((tq, LANES), F32), pltpu.VMEM((tq, LANES), F32),
                        pltpu.VMEM((tq, MLA_V), F32), pltpu.VMEM((tq, LANES), F32),
                        pltpu.VMEM((tq, LANES), F32)],
        grid=(b, MLA_HEADS, s // tq),
        in_specs=[pl.BlockSpec((None, tq, MLA_NOPE), lambda bb, h, i: (bb, i, h)),
                  pl.BlockSpec((None, tq, LANES), lambda bb, h, i: (bb, i, pe_blk0 + h // 2)),
                  pl.BlockSpec((None, s, MLA_KPAD), lambda bb, h, i: (bb, 0, h)),
                  pl.BlockSpec((None, s, MLA_V), lambda bb, h, i: (bb, 0, h)),
                  pl.BlockSpec((None, tq, MLA_V), lambda bb, h, i: (bb, i, h))],
        out_specs=pl.BlockSpec((None, tq, MLA_V), lambda bb, h, i: (bb, i, h)),
        out_shape=jax.ShapeDtypeStruct((b, s, MLA_HEADS * MLA_V), BF16),
        compiler_params=_cparams(("arbitrary", "arbitrary", "arbitrary")),
        name="mla_attn",
    )(q, q, k, v, gates)


def _dil_kernel(q_ref, kp_ref, kc_ref, vp_ref, vc_ref, o_ref, lse_ref):
    i = pl.program_id(2)
    qi = lax.broadcasted_iota(jnp.int32, (DIL_BLOCK, 2 * DIL_BLOCK), 0)
    kj = lax.broadcasted_iota(jnp.int32, (DIL_BLOCK, 2 * DIL_BLOCK), 1)
    valid = (kj >= qi) & (kj <= qi + DIL_BLOCK) & ((i > 0) | (kj >= DIL_BLOCK))
    lane = lax.broadcasted_iota(jnp.int32, (DIL_BLOCK, LANES), 1)
    stats = jnp.zeros((DIL_BLOCK, LANES), F32)
    for h in range(DIL_HEADS):
        hs = slice(h * DIL_HEAD_DIM, (h + 1) * DIL_HEAD_DIM)
        q = q_ref[:, hs]
        kk = jnp.concatenate([kp_ref[:, hs], kc_ref[:, hs]], axis=0)
        vv = jnp.concatenate([vp_ref[:, hs], vc_ref[:, hs]], axis=0)
        s = lax.dot_general(q, kk, (((1,), (1,)), ((), ())), preferred_element_type=F32)
        s = jnp.where(valid, s, NEG)
        m = jnp.max(s, axis=1, keepdims=True)
        p = jnp.exp2(s - m)
        l = jnp.sum(p, axis=1, keepdims=True)
        o = jnp.dot(p.astype(BF16), vv, preferred_element_type=F32) / l
        o_ref[:, hs] = o.astype(o_ref.dtype)
        stats = jnp.where(lane == h, m + jnp.log2(l), stats)
    lse_ref[...] = stats


def _dil_call(qkv, g):
    b, dilation, sd, _ = qkv.shape
    blk = lambda col, prev: pl.BlockSpec(
        (None, None, DIL_BLOCK, DIL_WIDTH),
        (lambda bb, r, i: (bb, r, jnp.maximum(i - 1, 0), col)) if prev
        else (lambda bb, r, i: (bb, r, i, col)))
    return pl.pallas_call(
        _dil_kernel,
        grid=(b, dilation, sd // DIL_BLOCK),
        in_specs=[blk(0, False), blk(1, True), blk(1, False), blk(2, True), blk(2, False)],
        out_specs=[pl.BlockSpec((None, None, DIL_BLOCK, DIL_WIDTH), lambda bb, r, i: (bb, r, i, 0)),
                   pl.BlockSpec((None, None, DIL_BLOCK, LANES), lambda bb, r, i: (bb, r, i, 0))],
        out_shape=[jax.ShapeDtypeStruct((b, dilation, sd, DIL_WIDTH), BF16),
                   jax.ShapeDtypeStruct((b, dilation, sd, LANES), F32)],
        compiler_params=_cparams(("arbitrary", "arbitrary", "arbitrary")),
        name="dil_attn_g%d" % g,
    )(qkv, qkv, qkv, qkv, qkv)


def _merge_kernel(a_ref, o0_ref, o1_ref, o2_ref, l0_ref, l1_ref, l2_ref, zd_ref, gm_ref, gd_ref,
                  wm_ref, wd_ref, out_ref, lse_scr, o_scr):
    o_refs = (o0_ref, o1_ref, o2_ref)
    l_refs = (l0_ref, l1_ref, l2_ref)
    lses, outs = [l0_ref[0]], [None]
    for g in range(1, DIL_N_GROUPS):
        dilation = o_refs[g].shape[0]
        rows = o_refs[g].shape[1]
        for r in range(dilation):
            rsl = pl.ds(r, rows, stride=dilation)
            lse_scr[g - 1, rsl, :] = l_refs[g][r]
            for h in range(DIL_HEADS):
                o_scr[g - 1, h, rsl, :] = (
                    o_refs[g][r, :, h * DIL_HEAD_DIM:(h + 1) * DIL_HEAD_DIM].astype(F32))
        lses.append(lse_scr[g - 1])
    l0, l1, l2 = lses
    mx = jnp.maximum(jnp.maximum(l0, l1), l2)
    e0, e1, e2 = jnp.exp2(l0 - mx), jnp.exp2(l1 - mx), jnp.exp2(l2 - mx)
    den = e0 + e1 + e2
    w0, w1, w2 = e0 / den, e1 / den, e2 / den
    parts = []
    for h in range(DIL_HEADS):
        hs = slice(h * DIL_HEAD_DIM, (h + 1) * DIL_HEAD_DIM)
        comb = (w0[:, h:h + 1] * o0_ref[0, :, hs].astype(F32)
                + w1[:, h:h + 1] * o_scr[0, h]
                + w2[:, h:h + 1] * o_scr[1, h])
        parts.append((comb * zd_ref[:, hs].astype(F32)).astype(BF16))
    bb = jnp.concatenate(parts, axis=1)
    y_mla = jnp.dot(a_ref[...], wm_ref[...], preferred_element_type=F32)
    y_dil = jnp.dot(bb, wd_ref[...], preferred_element_type=F32)
    merged = gm_ref[...].astype(F32) * y_mla + gd_ref[...].astype(F32) * y_dil
    out_ref[...] = merged.astype(out_ref.dtype)


def _merge_call(a, outs, lses, gates, w_mla_proj, w_dil_proj, tm=512):
    b, s, _ = a.shape
    d = D_MODEL
    row = lambda w, c: pl.BlockSpec((None, tm, w), lambda bb, i, c=c: (bb, i, c))
    res = lambda arr: pl.BlockSpec((None, arr.shape[1], tm // arr.shape[1], arr.shape[3]),
                                   lambda bb, i: (bb, 0, i, 0))
    const = lambda shp: pl.BlockSpec(shp, lambda bb, i: (0, 0))
    zd_blk = 3 * d // DIL_WIDTH
    return pl.pallas_call(
        _merge_kernel,
        grid=(b, s // tm),
        in_specs=[row(MLA_HEADS * MLA_V, 0)] + [res(o) for o in outs] + [res(l) for l in lses]
                 + [row(DIL_WIDTH, zd_blk), row(d, 1), row(d, 2),
                    const(w_mla_proj.shape), const(w_dil_proj.shape)],
        out_specs=pl.BlockSpec((None, tm, d), lambda bb, i: (bb, i, 0)),
        out_shape=jax.ShapeDtypeStruct((b, s, d), BF16),
        scratch_shapes=[pltpu.VMEM((DIL_N_GROUPS - 1, tm, LANES), F32),
                        pltpu.VMEM((DIL_N_GROUPS - 1, DIL_HEADS, tm, LANES), F32)],
        compiler_params=_cparams(("arbitrary", "arbitrary")),
        name="merge",
    )(a, *outs, *lses, gates, gates, gates, w_mla_proj, w_dil_proj)


def _final_kernel(mg_ref, w_ref, x_ref, gate_ref, g_ref, o_ref):
    out = jnp.dot(mg_ref[...], w_ref[...], preferred_element_type=F32)
    y = out * lax.rsqrt(jnp.mean(out * out, axis=-1, keepdims=True) + NORM_EPS) * g_ref[...]
    o_ref[...] = x_ref[...] + gate_ref[...] * y


def _final_call(merged, w_o, x2, gate3, g_post, s_len, tm=512):
    m, d = x2.shape
    per_b = s_len // tm
    return pl.pallas_call(
        _final_kernel,
        grid=(m // tm,),
        in_specs=[pl.BlockSpec((tm, d), lambda i: (i, 0)),
                  pl.BlockSpec(w_o.shape, lambda i: (0, 0)),
                  pl.BlockSpec((tm, d), lambda i: (i, 0)),
                  pl.BlockSpec((None, 1, d), lambda i: (i // per_b, 0, 0)),
                  pl.BlockSpec((1, d), lambda i: (0, 0))],
        out_specs=pl.BlockSpec((tm, d), lambda i: (i, 0)),
        out_shape=jax.ShapeDtypeStruct((m, d), F32),
        compiler_params=_cparams(("arbitrary",)),
        name="final",
    )(merged, w_o, x2, gate3, g_post)


def kernel(x, c, positions, w_ada, b_ada, g_pre, g_post, w_in, g_kv, w_kv_b, w_mla_proj, w_dil_proj, w_o):
    depth = w_in.shape[0]
    b, s, d = x.shape
    assert d == D_MODEL and s % (DIL_GROUPS[-1][1] * DIL_BLOCK) == 0
    m = b * s
    q_cols, kv_cols, g_cols, d_cols = _in_proj_columns()
    pad_rows = 8
    c_pad = jnp.zeros((pad_rows, d), F32).at[:b].set(c)
    tables = _tables_call(positions.reshape(m, 1))
    mla_tabs, dil_tabs = tables[:2], tables[2:]

    for layer in range(depth):
        mod = _ada_call(c_pad, w_ada[layer], b_ada[layer][None, :])[:b]
        shift, scale, gate = (mod[:, None, k * d:(k + 1) * d] for k in range(3))

        w = w_in[layer]
        w_q = _take_columns(w, q_cols).astype(BF16)
        w_kv = _take_columns(w, kv_cols).astype(BF16)
        w_g = _take_columns(w, g_cols).astype(BF16)
        w_d = [_take_columns(w, cols).astype(BF16) for cols in d_cols]
        wkb = w_kv_b[layer].reshape(MLA_KV_RANK, MLA_HEADS, MLA_NOPE + MLA_V)
        wkb = jnp.concatenate([wkb[:, :, :MLA_NOPE].reshape(MLA_KV_RANK, -1),
                               wkb[:, :, MLA_NOPE:].reshape(MLA_KV_RANK, -1)], axis=1).astype(BF16)

        x2 = x.reshape(m, d)
        h = _norm_call(x2, g_pre[layer][None, :], scale, shift, s)

        q = _proj_call(h, w_q, "q", 1024, BF16, mla_tabs)
        kv_a = _proj_call(h, w_kv, "kv", w_kv.shape[1], F32)
        gates = _proj_call(h, w_g, "gates", 1024, BF16).reshape(b, s, -1)

        k_mla, v_mla = _kvb_call(kv_a, g_kv[layer][None, :], wkb, mla_tabs)
        a = _mla_call(q.reshape(b, s, -1), k_mla.reshape(b, s, -1), v_mla.reshape(b, s, -1), gates)

        h3 = h.reshape(b, s, d)
        dil_tabs3 = [t.reshape(b, s, LANES) for t in dil_tabs]
        outs, lses = [], []
        for g, (_, dilation) in enumerate(DIL_GROUPS):
            qkv_g = _dilproj_call(h3, w_d[g], dil_tabs3, dilation)
            o_g, lse_g = _dil_call(qkv_g, g)
            outs.append(o_g)
            lses.append(lse_g)

        merged = _merge_call(a, outs, lses, gates,
                             w_mla_proj[layer].astype(BF16), w_dil_proj[layer].astype(BF16))
        x = _final_call(merged.reshape(m, d), w_o[layer].astype(BF16), x2, gate,
                        g_post[layer][None, :], s).reshape(b, s, d)
    return x
```

```python
import functools
import math

import numpy as np
import jax
import jax.numpy as jnp
from jax import lax
from jax.experimental import pallas as pl
from jax.experimental.pallas import tpu as pltpu

D_MODEL = 2048
ROPE_THETA = 500000.0
NORM_EPS = 1e-6

MLA_HEADS = 16
MLA_NOPE = 128
MLA_ROPE = 64
MLA_V = 128
MLA_QK = MLA_NOPE + MLA_ROPE
MLA_KV_RANK = 512
MLA_KPAD = 256

DIL_GROUPS = ((128, 1), (512, 4), (2048, 16))
DIL_N_GROUPS = 3
DIL_HEADS = 8
DIL_HEAD_DIM = 128
DIL_ROT = 32
DIL_WIDTH = DIL_HEADS * DIL_HEAD_DIM
DIL_BLOCK = 128

LANES = 128
HALF = LANES // 2
NEG = -1e30
LOG2E = math.log2(math.e)
VMEM_LIMIT = 56 * 1024 * 1024

F32 = jnp.float32
BF16 = jnp.bfloat16


def _cparams(sem, flags=None):
    return pltpu.CompilerParams(dimension_semantics=sem, vmem_limit_bytes=VMEM_LIMIT, flags=flags)


def _in_proj_columns():
    q0 = 0
    kv0 = MLA_HEADS * MLA_QK
    z_mla0 = kv0 + MLA_KV_RANK + MLA_ROPE
    dil0 = z_mla0 + MLA_HEADS * MLA_V
    z_dil0 = dil0 + 3 * DIL_N_GROUPS * DIL_WIDTH
    gl_mla0 = z_dil0 + DIL_WIDTH
    gl_dil0 = gl_mla0 + D_MODEL
    half = MLA_ROPE // 2

    q_cols = []
    for h in range(MLA_HEADS):
        q_cols.append(q0 + h * MLA_QK + np.arange(MLA_NOPE))
    for j in range(MLA_HEADS // 2):
        a, b = 2 * j, 2 * j + 1
        for part in (0, 1):
            for hh in (a, b):
                q_cols.append(q0 + hh * MLA_QK + MLA_NOPE + part * half + np.arange(half))
    q_cols = np.concatenate(q_cols)

    kx1 = kv0 + MLA_KV_RANK + np.arange(half)
    kx2 = kx1 + half
    kv_cols = np.concatenate([kv0 + np.arange(MLA_KV_RANK), kx1, kx1, kx2, kx2])

    g_cols = np.concatenate([z_mla0 + np.arange(MLA_HEADS * MLA_V), gl_mla0 + np.arange(D_MODEL),
                             gl_dil0 + np.arange(D_MODEL), z_dil0 + np.arange(DIL_WIDTH)])

    rh = DIL_ROT // 2
    perm = np.concatenate([np.arange(0, rh), np.arange(DIL_ROT, DIL_ROT + HALF - rh),
                           np.arange(rh, DIL_ROT), np.arange(DIL_ROT + HALF - rh, DIL_HEAD_DIM)])
    n_heads = DIL_N_GROUPS * DIL_HEADS
    d_cols = []
    for g in range(DIL_N_GROUPS):
        cols = []
        for part in range(3):
            for hh in range(g * DIL_HEADS, (g + 1) * DIL_HEADS):
                base = dil0 + (part * n_heads + hh) * DIL_HEAD_DIM
                cols.append(base + (perm if part < 2 else np.arange(DIL_HEAD_DIM)))
        d_cols.append(np.concatenate(cols))
    return q_cols, kv_cols, g_cols, d_cols


def _take_columns(w, cols):
    breaks = np.flatnonzero(np.diff(cols) != 1) + 1
    starts = np.concatenate([[0], breaks])
    ends = np.concatenate([breaks, [len(cols)]])
    return jnp.concatenate([w[:, int(cols[a]):int(cols[a]) + int(e - a)] for a, e in zip(starts, ends)],
                           axis=1)


def _rope_lane_rows():
    inv64 = 1.0 / (ROPE_THETA ** (jnp.arange(0, MLA_ROPE, 2, dtype=F32) / MLA_ROPE))
    inv32 = 1.0 / (ROPE_THETA ** (jnp.arange(0, DIL_ROT, 2, dtype=F32) / DIL_ROT))
    mla_f = jnp.tile(inv64, 4)[None, :]
    z = jnp.zeros((HALF - DIL_ROT // 2,), F32)
    dil_f = jnp.concatenate([inv32, z, inv32, z])[None, :]
    sign = jnp.concatenate([-jnp.ones((HALF,), F32), jnp.ones((HALF,), F32)])[None, :]
    return mla_f, dil_f, sign


def _ada_kernel(c_ref, w_ref, b_ref, o_ref):
    c = c_ref[...]
    sc = c * (1.0 / (1.0 + jnp.exp(-c)))
    o_ref[...] = jnp.dot(sc, w_ref[...], preferred_element_type=F32) + b_ref[...]


def _ada_call(c_pad, w_ada, b_ada, tn=512):
    rows, d = c_pad.shape
    n = w_ada.shape[1]
    return pl.pallas_call(
        _ada_kernel,
        grid=(n // tn,),
        in_specs=[pl.BlockSpec((rows, d), lambda j: (0, 0)),
                  pl.BlockSpec((d, tn), lambda j: (0, j)),
                  pl.BlockSpec((1, tn), lambda j: (0, j))],
        out_specs=pl.BlockSpec((rows, tn), lambda j: (0, j)),
        out_shape=jax.ShapeDtypeStruct((rows, n), F32),
        compiler_params=_cparams(("arbitrary",)),
        name="ada",
    )(c_pad, w_ada, b_ada)


def _tables_kernel(pos_ref, mf_ref, df_ref, sg_ref, mc_ref, ms_ref, dc_ref, ds_ref):
    pos = pos_ref[...].astype(F32)
    sg = sg_ref[...]
    ang = pos * mf_ref[...]
    mc_ref[...] = jnp.cos(ang)
    ms_ref[...] = jnp.sin(ang) * sg
    ang = pos * df_ref[...]
    dc_ref[...] = jnp.cos(ang)
    ds_ref[...] = jnp.sin(ang) * sg


def _tables_call(pos_col, tm=1024):
    m = pos_col.shape[0]
    mla_f, dil_f, sign = _rope_lane_rows()
    row = pl.BlockSpec((1, LANES), lambda i: (0, 0))
    tab = pl.BlockSpec((tm, LANES), lambda i: (i, 0))
    shp = jax.ShapeDtypeStruct((m, LANES), F32)
    return pl.pallas_call(
        _tables_kernel,
        grid=(m // tm,),
        in_specs=[pl.BlockSpec((tm, 1), lambda i: (i, 0)), row, row, row],
        out_specs=[tab, tab, tab, tab],
        out_shape=[shp, shp, shp, shp],
        compiler_params=_cparams(("arbitrary",)),
        name="rope_tables",
    )(pos_col, mla_f, dil_f, sign)


def _norm_kernel(x_ref, g_ref, sc_ref, sh_ref, o_ref):
    x = x_ref[...]
    y = x * lax.rsqrt(jnp.mean(x * x, axis=-1, keepdims=True) + NORM_EPS) * g_ref[...]
    o_ref[...] = (y * (1.0 + sc_ref[...]) + sh_ref[...]).astype(o_ref.dtype)


def _norm_call(x2, g_pre, scale3, shift3, s_len, tm=512):
    m, d = x2.shape
    per_b = s_len // tm
    vec = pl.BlockSpec((None, 1, d), lambda i: (i // per_b, 0, 0))
    return pl.pallas_call(
        _norm_kernel,
        grid=(m // tm,),
        in_specs=[pl.BlockSpec((tm, d), lambda i: (i, 0)),
                  pl.BlockSpec((1, d), lambda i: (0, 0)), vec, vec],
        out_specs=pl.BlockSpec((tm, d), lambda i: (i, 0)),
        out_shape=jax.ShapeDtypeStruct((m, d), BF16),
        compiler_params=_cparams(("arbitrary",)),
        name="norm_mod",
    )(x2, g_pre, scale3, shift3)


def _rotate_groups(acc, c_ref, s_ref, o_ref, scale):
    cos = c_ref[...]
    sin = s_ref[...]
    for g in range(acc.shape[1] // LANES):
        xg = acc[:, g * LANES:(g + 1) * LANES]
        r = xg * cos + pltpu.roll(xg, HALF, 1) * sin
        if scale != 1.0:
            r = r * scale
        o_ref[:, g * LANES:(g + 1) * LANES] = r.astype(o_ref.dtype)


def _sigmoid(z):
    return 1.0 / (1.0 + jnp.exp(-z))


def _proj_kernel(*refs, mode, tn):
    if mode == "q":
        h_ref, w_ref, c_ref, s_ref, o_ref = refs
    else:
        h_ref, w_ref, o_ref = refs
    j = pl.program_id(1)
    acc = jnp.dot(h_ref[...], w_ref[...], preferred_element_type=F32)

    if mode == "kv":
        o_ref[...] = acc
    elif mode == "q":
        scale = (MLA_QK ** -0.5) * LOG2E
        n_nope = MLA_HEADS * MLA_NOPE // tn

        @pl.when(j < n_nope)
        def _():
            o_ref[...] = (acc * scale).astype(o_ref.dtype)

        @pl.when(j >= n_nope)
        def _():
            _rotate_groups(acc, c_ref, s_ref, o_ref, scale)
    elif mode == "gates":
        n_z_mla = MLA_HEADS * MLA_V // tn
        n_sig = 2 * D_MODEL // tn
        sg = _sigmoid(acc)

        @pl.when((j < n_z_mla) | (j >= n_z_mla + n_sig))
        def _():
            o_ref[...] = (acc * sg).astype(o_ref.dtype)

        @pl.when((j >= n_z_mla) & (j < n_z_mla + n_sig))
        def _():
            o_ref[...] = sg.astype(o_ref.dtype)


def _dilproj_kernel(h_ref, w_ref, c_ref, s_ref, o_ref, *scratch, dilation):
    j = pl.program_id(2)
    tm = h_ref.shape[0]
    acc = jnp.dot(h_ref[...], w_ref[...], preferred_element_type=F32)
    n_groups = acc.shape[1] // LANES
    scale = (DIL_HEAD_DIM ** -0.5) * LOG2E

    def emit(g, val):
        if dilation == 1:
            o_ref[0, :, g * LANES:(g + 1) * LANES] = val.astype(o_ref.dtype)
        else:
            scratch[0][g] = val

    def rotated(mult):
        cos = c_ref[...]
        sin = s_ref[...]
        for g in range(n_groups):
            xg = acc[:, g * LANES:(g + 1) * LANES]
            r = xg * cos + pltpu.roll(xg, HALF, 1) * sin
            emit(g, r * mult if mult != 1.0 else r)

    @pl.when(j == 0)
    def _():
        rotated(scale)

    @pl.when(j == 1)
    def _():
        rotated(1.0)

    @pl.when(j == 2)
    def _():
        for g in range(n_groups):
            emit(g, acc[:, g * LANES:(g + 1) * LANES])

    if dilation > 1:
        rows = tm // dilation
        for r in range(dilation):
            for g in range(n_groups):
                o_ref[r, :, g * LANES:(g + 1) * LANES] = (
                    scratch[0][g, pl.ds(r, rows, stride=dilation), :].astype(o_ref.dtype))


def _dilproj_call(h3, w, tables3, dilation, tm=1024):
    b, s, k = h3.shape
    n = w.shape[1]
    tn = DIL_WIDTH
    scratch = [] if dilation == 1 else [pltpu.VMEM((tn // LANES, tm, LANES), F32)]
    tab = pl.BlockSpec((None, tm, LANES), lambda bb, i, j: (bb, i, 0))
    return pl.pallas_call(
        functools.partial(_dilproj_kernel, dilation=dilation),
        grid=(b, s // tm, n // tn),
        in_specs=[pl.BlockSpec((None, tm, k), lambda bb, i, j: (bb, i, 0)),
                  pl.BlockSpec((k, tn), lambda bb, i, j: (0, j)), tab, tab],
        out_specs=pl.BlockSpec((None, dilation, tm // dilation, tn), lambda bb, i, j: (bb, 0, i, j)),
        out_shape=jax.ShapeDtypeStruct((b, dilation, s // dilation, n), BF16),
        scratch_shapes=scratch,
        compiler_params=_cparams(("arbitrary", "arbitrary", "arbitrary")),
        name="proj_dil_d%d" % dilation,
    )(h3, w, *tables3)


def _proj_call(h, w, mode, tn, out_dtype, tables=None, tm=1024):
    m, k = h.shape
    n = w.shape[1]
    in_specs = [pl.BlockSpec((tm, k), lambda i, j: (i, 0)),
                pl.BlockSpec((k, tn), lambda i, j: (0, j))]
    args = [h, w]
    if tables is not None:
        tab = pl.BlockSpec((tm, LANES), lambda i, j: (i, 0))
        in_specs += [tab, tab]
        args += list(tables)
    return pl.pallas_call(
        functools.partial(_proj_kernel, mode=mode, tn=tn),
        grid=(m // tm, n // tn),
        in_specs=in_specs,
        out_specs=pl.BlockSpec((tm, tn), lambda i, j: (i, j)),
        out_shape=jax.ShapeDtypeStruct((m, n), out_dtype),
        compiler_params=_cparams(("arbitrary", "arbitrary")),
        name="proj_" + mode,
    )(*args)


def _kvb_kernel(kv_ref, g_ref, w_ref, c_ref, s_ref, k_ref, v_ref):
    kv = kv_ref[...]
    lat = kv[:, :MLA_KV_RANK]
    lat = lat * lax.rsqrt(jnp.mean(lat * lat, axis=-1, keepdims=True) + NORM_EPS) * g_ref[...]
    kvb = jnp.dot(lat.astype(BF16), w_ref[...], preferred_element_type=F32)
    pe = kv[:, MLA_KV_RANK:]
    pe = pe * c_ref[...] + pltpu.roll(pe, HALF, 1) * s_ref[...]
    lane = lax.broadcasted_iota(jnp.int32, pe.shape, 1)
    first = (lane % HALF) < (HALF // 2)
    pe_even = jnp.where(first, pe, 0.0).astype(k_ref.dtype)
    pe_odd = jnp.where(first, 0.0, pe).astype(k_ref.dtype)
    for h in range(MLA_HEADS):
        k_ref[:, h * MLA_KPAD:h * MLA_KPAD + MLA_NOPE] = (
            kvb[:, h * MLA_NOPE:(h + 1) * MLA_NOPE].astype(k_ref.dtype))
        k_ref[:, h * MLA_KPAD + MLA_NOPE:(h + 1) * MLA_KPAD] = pe_even if h % 2 == 0 else pe_odd
    v_ref[...] = kvb[:, MLA_HEADS * MLA_NOPE:].astype(v_ref.dtype)


def _kvb_call(kv_a, g_kv, w_kvb, tables, tm=256):
    m, kw = kv_a.shape
    nk = MLA_HEADS * MLA_KPAD
    nv = MLA_HEADS * MLA_V
    tab = pl.BlockSpec((tm, LANES), lambda i: (i, 0))
    return pl.pallas_call(
        _kvb_kernel,
        grid=(m // tm,),
        in_specs=[pl.BlockSpec((tm, kw), lambda i: (i, 0)),
                  pl.BlockSpec((1, MLA_KV_RANK), lambda i: (0, 0)),
                  pl.BlockSpec(w_kvb.shape, lambda i: (0, 0)), tab, tab],
        out_specs=[pl.BlockSpec((tm, nk), lambda i: (i, 0)),
                   pl.BlockSpec((tm, nv), lambda i: (i, 0))],
        out_shape=[jax.ShapeDtypeStruct((m, nk), BF16), jax.ShapeDtypeStruct((m, nv), BF16)],
        compiler_params=_cparams(("arbitrary",)),
        name="kvb",
    )(kv_a, g_kv, w_kvb, *tables)


def _mla_kernel(qn_ref, qp_ref, k_ref, v_ref, z_ref, o_ref,
                s0_ref, s1_ref, p0_ref, p1_ref, a0_ref, a1_ref, acc_ref, m_ref, *, tq, rb):
    i = pl.program_id(2)
    q = jnp.concatenate([qn_ref[...], qp_ref[...]], axis=1)
    s_bufs = (s0_ref, s1_ref)
    p_bufs = (p0_ref, p1_ref)
    a_bufs = (a0_ref, a1_ref)
    nt = (((1,), (1,)), ((), ()))
    half = tq // 2

    def keys(ref, off, n):
        return ref[pl.ds(off if isinstance(off, int) else pl.multiple_of(off, tq), n), :]

    def scores(off, slot, row0=0, row1=tq):
        s_bufs[slot][row0:row1, :] = lax.dot_general(q[row0:row1], keys(k_ref, off, tq), nt,
                                                     preferred_element_type=F32)

    def softmax(slot, row0=0, row1=tq, ncols=tq, causal=False):
        for r0 in range(row0, row1, rb):
            rs = slice(r0, r0 + rb)
            s = s_bufs[slot][rs, :ncols]
            if causal and ncols - 1 > r0:
                rows = r0 + lax.broadcasted_iota(jnp.int32, (rb, ncols), 0)
                cols = lax.broadcasted_iota(jnp.int32, (rb, ncols), 1)
                s = jnp.where(cols <= rows, s, NEG)
            m_old = m_ref[rs, :]
            m_new = jnp.maximum(m_old, jnp.max(s, axis=1, keepdims=True))
            for c in range(ncols // LANES):
                cs = slice(c * LANES, (c + 1) * LANES)
                p_bufs[slot][rs, cs] = jnp.exp2((s[:, cs] - m_new).astype(BF16))
            m_ref[rs, :] = m_new
            a_bufs[slot][rs, :] = jnp.exp2(m_old - m_new)

    def ones_col(n):
        lane = lax.broadcasted_iota(jnp.int32, (n, LANES), 1)
        return jnp.where(lane == 0, 1.0, 0.0).astype(BF16)

    def pv(off, slot, row0=0, row1=tq, ncols=tq):
        v_aug = jnp.concatenate([keys(v_ref, off, ncols), ones_col(ncols)], axis=1)
        upd = jnp.dot(p_bufs[slot][row0:row1, :ncols], v_aug, preferred_element_type=F32)
        alpha = a_bufs[slot][row0:row1, :]
        acc_ref[row0:row1, :] = jnp.concatenate([alpha, alpha], axis=1) * acc_ref[row0:row1, :] + upd

    def step(next_off, cur_off, cur):
        scores(next_off, 1 - cur)
        softmax(cur)
        pv(cur_off, cur)

    def diagonal(off, slot):
        softmax(slot, 0, half, ncols=half, causal=True)
        pv(off, slot, 0, half, ncols=half)
        softmax(slot, half, tq, causal=True)
        pv(off, slot, half, tq)

    m_ref[...] = jnp.full(m_ref.shape, NEG, F32)
    acc_ref[...] = jnp.zeros(acc_ref.shape, F32)

    scores(0, 0)
    n_pairs = lax.shift_right_logical(i, 1)

    @pl.loop(0, n_pairs)
    def _(t):
        base = pl.multiple_of(t * (2 * tq), tq)
        step(base + tq, base, 0)
        step(base + 2 * tq, base + tq, 1)

    base = pl.multiple_of(n_pairs * (2 * tq), tq)

    @pl.when(i % 2 == 1)
    def _():
        step(base + tq, base, 0)
        diagonal(base + tq, 1)

    @pl.when(i % 2 == 0)
    def _():
        diagonal(base, 0)

    acc = acc_ref[...]
    o_ref[...] = (acc[:, :MLA_V] / acc[:, MLA_V:MLA_V + 1] * z_ref[...].astype(F32)).astype(o_ref.dtype)


def _mla_call(q, k, v, gates, tq=1024, rb=16):
    b, s, _ = q.shape
    assert s % tq == 0 and tq % (2 * rb) == 0
    pe_blk0 = MLA_HEADS * MLA_NOPE // LANES
    return pl.pallas_call(
        functools.partial(_mla_kernel, tq=tq, rb=rb),
        scratch_shapes=[pltpu.VMEM((tq, tq), F32), pltpu.VMEM((tq, tq), F32),
                        pltpu.VMEM((tq, tq), BF16), pltpu.VMEM((tq, tq), BF16),
                        pltpu.VMEM((tq, LANES), F32), pltpu.VMEM((tq, LANES), F32),
                        pltpu.VMEM((tq, MLA_V + LANES), F32), pltpu.VMEM((tq, LANES), F32)],
        grid=(b, MLA_HEADS, s // tq),
        in_specs=[pl.BlockSpec((None, tq, MLA_NOPE), lambda bb, h, i: (bb, i, h)),
                  pl.BlockSpec((None, tq, LANES), lambda bb, h, i: (bb, i, pe_blk0 + h // 2)),
                  pl.BlockSpec((None, s, MLA_KPAD), lambda bb, h, i: (bb, 0, h)),
                  pl.BlockSpec((None, s, MLA_V), lambda bb, h, i: (bb, 0, h)),
                  pl.BlockSpec((None, tq, MLA_V), lambda bb, h, i: (bb, i, h))],
        out_specs=pl.BlockSpec((None, tq, MLA_V), lambda bb, h, i: (bb, i, h)),
        out_shape=jax.ShapeDtypeStruct((b, s, MLA_HEADS * MLA_V), BF16),
        compiler_params=_cparams(("arbitrary", "arbitrary", "arbitrary")),
        name="mla_attn",
    )(q, q, k, v, gates)


def _dil_kernel(q_ref, kp_ref, kc_ref, vp_ref, vc_ref, o_ref, lse_ref):
    i = pl.program_id(2)
    qi = lax.broadcasted_iota(jnp.int32, (DIL_BLOCK, 2 * DIL_BLOCK), 0)
    kj = lax.broadcasted_iota(jnp.int32, (DIL_BLOCK, 2 * DIL_BLOCK), 1)
    valid = (kj >= qi) & (kj <= qi + DIL_BLOCK) & ((i > 0) | (kj >= DIL_BLOCK))
    lane = lax.broadcasted_iota(jnp.int32, (DIL_BLOCK, LANES), 1)
    stats = jnp.zeros((DIL_BLOCK, LANES), F32)
    for h in range(DIL_HEADS):
        hs = slice(h * DIL_HEAD_DIM, (h + 1) * DIL_HEAD_DIM)
        q = q_ref[:, hs]
        kk = jnp.concatenate([kp_ref[:, hs], kc_ref[:, hs]], axis=0)
        vv = jnp.concatenate([vp_ref[:, hs], vc_ref[:, hs]], axis=0)
        s = lax.dot_general(q, kk, (((1,), (1,)), ((), ())), preferred_element_type=F32)
        s = jnp.where(valid, s, NEG)
        m = jnp.max(s, axis=1, keepdims=True)
        p = jnp.exp2(s - m)
        l = jnp.sum(p, axis=1, keepdims=True)
        o = jnp.dot(p.astype(BF16), vv, preferred_element_type=F32) / l
        o_ref[:, hs] = o.astype(o_ref.dtype)
        stats = jnp.where(lane == h, m + jnp.log2(l), stats)
    lse_ref[...] = stats


def _dil_call(qkv, g):
    b, dilation, sd, _ = qkv.shape
    blk = lambda col, prev: pl.BlockSpec(
        (None, None, DIL_BLOCK, DIL_WIDTH),
        (lambda bb, r, i: (bb, r, jnp.maximum(i - 1, 0), col)) if prev
        else (lambda bb, r, i: (bb, r, i, col)))
    return pl.pallas_call(
        _dil_kernel,
        grid=(b, dilation, sd // DIL_BLOCK),
        in_specs=[blk(0, False), blk(1, True), blk(1, False), blk(2, True), blk(2, False)],
        out_specs=[pl.BlockSpec((None, None, DIL_BLOCK, DIL_WIDTH), lambda bb, r, i: (bb, r, i, 0)),
                   pl.BlockSpec((None, None, DIL_BLOCK, LANES), lambda bb, r, i: (bb, r, i, 0))],
        out_shape=[jax.ShapeDtypeStruct((b, dilation, sd, DIL_WIDTH), BF16),
                   jax.ShapeDtypeStruct((b, dilation, sd, LANES), F32)],
        compiler_params=_cparams(("arbitrary", "arbitrary", "arbitrary")),
        name="dil_attn_g%d" % g,
    )(qkv, qkv, qkv, qkv, qkv)


def _merge_kernel(a_ref, o0_ref, o1_ref, o2_ref, l0_ref, l1_ref, l2_ref, zd_ref, gm_ref, gd_ref,
                  wm_ref, wd_ref, out_ref, lse_scr, o_scr):
    o_refs = (o0_ref, o1_ref, o2_ref)
    l_refs = (l0_ref, l1_ref, l2_ref)
    lses, outs = [l0_ref[0]], [None]
    for g in range(1, DIL_N_GROUPS):
        dilation = o_refs[g].shape[0]
        rows = o_refs[g].shape[1]
        for r in range(dilation):
            rsl = pl.ds(r, rows, stride=dilation)
            lse_scr[g - 1, rsl, :] = l_refs[g][r]
            for h in range(DIL_HEADS):
                o_scr[g - 1, h, rsl, :] = (
                    o_refs[g][r, :, h * DIL_HEAD_DIM:(h + 1) * DIL_HEAD_DIM].astype(F32))
        lses.append(lse_scr[g - 1])
    l0, l1, l2 = lses
    mx = jnp.maximum(jnp.maximum(l0, l1), l2)
    e0, e1, e2 = jnp.exp2(l0 - mx), jnp.exp2(l1 - mx), jnp.exp2(l2 - mx)
    den = e0 + e1 + e2
    w0, w1, w2 = e0 / den, e1 / den, e2 / den
    parts = []
    for h in range(DIL_HEADS):
        hs = slice(h * DIL_HEAD_DIM, (h + 1) * DIL_HEAD_DIM)
        comb = (w0[:, h:h + 1] * o0_ref[0, :, hs].astype(F32)
                + w1[:, h:h + 1] * o_scr[0, h]
                + w2[:, h:h + 1] * o_scr[1, h])
        parts.append((comb * zd_ref[:, hs].astype(F32)).astype(BF16))
    bb = jnp.concatenate(parts, axis=1)
    y_mla = jnp.dot(a_ref[...], wm_ref[...], preferred_element_type=F32)
    y_dil = jnp.dot(bb, wd_ref[...], preferred_element_type=F32)
    merged = gm_ref[...].astype(F32) * y_mla + gd_ref[...].astype(F32) * y_dil
    out_ref[...] = merged.astype(out_ref.dtype)


def _merge_call(a, outs, lses, gates, w_mla_proj, w_dil_proj, tm=512):
    b, s, _ = a.shape
    d = D_MODEL
    row = lambda w, c: pl.BlockSpec((None, tm, w), lambda bb, i, c=c: (bb, i, c))
    res = lambda arr: pl.BlockSpec((None, arr.shape[1], tm // arr.shape[1], arr.shape[3]),
                                   lambda bb, i: (bb, 0, i, 0))
    const = lambda shp: pl.BlockSpec(shp, lambda bb, i: (0, 0))
    zd_blk = 3 * d // DIL_WIDTH
    return pl.pallas_call(
        _merge_kernel,
        grid=(b, s // tm),
        in_specs=[row(MLA_HEADS * MLA_V, 0)] + [res(o) for o in outs] + [res(l) for l in lses]
                 + [row(DIL_WIDTH, zd_blk), row(d, 1), row(d, 2),
                    const(w_mla_proj.shape), const(w_dil_proj.shape)],
        out_specs=pl.BlockSpec((None, tm, d), lambda bb, i: (bb, i, 0)),
        out_shape=jax.ShapeDtypeStruct((b, s, d), BF16),
        scratch_shapes=[pltpu.VMEM((DIL_N_GROUPS - 1, tm, LANES), F32),
                        pltpu.VMEM((DIL_N_GROUPS - 1, DIL_HEADS, tm, LANES), F32)],
        compiler_params=_cparams(("arbitrary", "arbitrary")),
        name="merge",
    )(a, *outs, *lses, gates, gates, gates, w_mla_proj, w_dil_proj)


def _final_kernel(mg_ref, w_ref, x_ref, gate_ref, g_ref, o_ref):
    out = jnp.dot(mg_ref[...], w_ref[...], preferred_element_type=F32)
    y = out * lax.rsqrt(jnp.mean(out * out, axis=-1, keepdims=True) + NORM_EPS) * g_ref[...]
    o_ref[...] = x_ref[...] + gate_ref[...] * y


def _final_call(merged, w_o, x2, gate3, g_post, s_len, tm=512):
    m, d = x2.shape
    per_b = s_len // tm
    return pl.pallas_call(
        _final_kernel,
        grid=(m // tm,),
        in_specs=[pl.BlockSpec((tm, d), lambda i: (i, 0)),
                  pl.BlockSpec(w_o.shape, lambda i: (0, 0)),
                  pl.BlockSpec((tm, d), lambda i: (i, 0)),
                  pl.BlockSpec((None, 1, d), lambda i: (i // per_b, 0, 0)),
                  pl.BlockSpec((1, d), lambda i: (0, 0))],
        out_specs=pl.BlockSpec((tm, d), lambda i: (i, 0)),
        out_shape=jax.ShapeDtypeStruct((m, d), F32),
        compiler_params=_cparams(("arbitrary",)),
        name="final",
    )(merged, w_o, x2, gate3, g_post)


def kernel(x, c, positions, w_ada, b_ada, g_pre, g_post, w_in, g_kv, w_kv_b, w_mla_proj, w_dil_proj, w_o):
    depth = w_in.shape[0]
    b, s, d = x.shape
    assert d == D_MODEL and s % (DIL_GROUPS[-1][1] * DIL_BLOCK) == 0
    m = b * s
    q_cols, kv_cols, g_cols, d_cols = _in_proj_columns()
    pad_rows = 8
    c_pad = jnp.zeros((pad_rows, d), F32).at[:b].set(c)
    tables = _tables_call(positions.reshape(m, 1))
    mla_tabs, dil_tabs = tables[:2], tables[2:]

    for layer in range(depth):
        mod = _ada_call(c_pad, w_ada[layer], b_ada[layer][None, :])[:b]
        shift, scale, gate = (mod[:, None, k * d:(k + 1) * d] for k in range(3))

        w = w_in[layer]
        w_q = _take_columns(w, q_cols).astype(BF16)
        w_kv = _take_columns(w, kv_cols).astype(BF16)
        w_g = _take_columns(w, g_cols).astype(BF16)
        w_d = [_take_columns(w, cols).astype(BF16) for cols in d_cols]
        wkb = w_kv_b[layer].reshape(MLA_KV_RANK, MLA_HEADS, MLA_NOPE + MLA_V)
        wkb = jnp.concatenate([wkb[:, :, :MLA_NOPE].reshape(MLA_KV_RANK, -1),
                               wkb[:, :, MLA_NOPE:].reshape(MLA_KV_RANK, -1)], axis=1).astype(BF16)

        x2 = x.reshape(m, d)
        h = _norm_call(x2, g_pre[layer][None, :], scale, shift, s)

        q = _proj_call(h, w_q, "q", 1024, BF16, mla_tabs)
        kv_a = _proj_call(h, w_kv, "kv", w_kv.shape[1], F32)
        gates = _proj_call(h, w_g, "gates", 1024, BF16).reshape(b, s, -1)

        k_mla, v_mla = _kvb_call(kv_a, g_kv[layer][None, :], wkb, mla_tabs)
        a = _mla_call(q.reshape(b, s, -1), k_mla.reshape(b, s, -1), v_mla.reshape(b, s, -1), gates)

        h3 = h.reshape(b, s, d)
        dil_tabs3 = [t.reshape(b, s, LANES) for t in dil_tabs]
        outs, lses = [], []
        for g, (_, dilation) in enumerate(DIL_GROUPS):
            qkv_g = _dilproj_call(h3, w_d[g], dil_tabs3, dilation)
            o_g, lse_g = _dil_call(qkv_g, g)
            outs.append(o_g)
            lses.append(lse_g)

        merged = _merge_call(a, outs, lses, gates,
                             w_mla_proj[layer].astype(BF16), w_dil_proj[layer].astype(BF16))
        x = _final_call(merged.reshape(m, d), w_o[layer].astype(BF16), x2, gate,
                        g_post[layer][None, :], s).reshape(b, s, d)
    return x
```
